```python
import math
import jax, jax.numpy as jnp
from jax import lax
import numpy as np

D_MODEL = 2048
BATCH = 8
SEQ = 2048
DEPTH = 2

ATT_HEADS = 8
ATT_HEAD_DIM = 64
ATT_V_DIM = 2 * ATT_HEAD_DIM
ATT_QK_WIDTH = ATT_HEADS * 2 * ATT_HEAD_DIM
ATT_WIDTH = ATT_HEADS * ATT_V_DIM
LRU_WIDTH = D_MODEL // 2
LRU_BLOCKS = 8
LRU_BLOCK_W = LRU_WIDTH // LRU_BLOCKS
CONV_WIDTH = 4
LRU_C = 8.0
Q_BLOCK = 128
LN_EPS = 1e-5
SUBLN_EPS = 1e-5
DN_ALPHA = (2 * DEPTH) ** 0.25
DN_BETA = (8 * DEPTH) ** -0.25

IN_WIDTHS = [ATT_QK_WIDTH, ATT_QK_WIDTH, ATT_WIDTH, ATT_WIDTH,
             LRU_WIDTH, LRU_WIDTH, D_MODEL, D_MODEL]
IN_TOTAL = int(sum(IN_WIDTHS))
SPLIT_IDX = [int(v) for v in np.cumsum(IN_WIDTHS)[:-1]]

kernel_name = "hybrid_diffattn_rglru_deepnorm"


def _layer_norm(x, g, b):
    xf = x.astype(jnp.float32)
    mu = jnp.mean(xf, axis=-1, keepdims=True)
    var = jnp.mean(jnp.square(xf - mu), axis=-1, keepdims=True)
    y = (xf - mu) * lax.rsqrt(var + LN_EPS) * g.astype(jnp.float32) + b.astype(jnp.float32)
    return y.astype(x.dtype)


def _diff_attention(q, k, v, lam, lam_init, subln_g):
    B, S = q.shape[0], q.shape[1]
    nb = S // Q_BLOCK
    scale = ATT_HEAD_DIM ** -0.5
    slopes = jnp.exp2(-(8.0 / ATT_HEADS) * jnp.arange(1, ATT_HEADS + 1, dtype=jnp.float32))
    kpos = jnp.arange(S)
    vf = v.astype(jnp.float32)
    qb = q.reshape(B, nb, Q_BLOCK, ATT_HEADS, 2, ATT_HEAD_DIM).transpose(1, 0, 2, 3, 4, 5)

    def block(args):
        qi, i = args
        qpos = i * Q_BLOCK + jnp.arange(Q_BLOCK)
        dist = (qpos[:, None] - kpos[None, :]).astype(jnp.float32)
        bias = jnp.where(dist >= 0, -slopes[:, None, None] * dist, -jnp.inf)
        s = jnp.einsum('bqhcd,bkhcd->bchqk', qi, k,
                       preferred_element_type=jnp.float32) * scale + bias
        p = jax.nn.softmax(s, axis=-1)
        w = p[:, 0] - lam * p[:, 1]
        return jnp.einsum('bhqk,bkhe->bqhe', w, vf)

    o = lax.map(block, (qb, jnp.arange(nb)))
    o = o.transpose(1, 0, 2, 3, 4).reshape(B, S, ATT_HEADS, ATT_V_DIM)
    o = o * lax.rsqrt(jnp.mean(jnp.square(o), axis=-1, keepdims=True) + SUBLN_EPS)
    o = o * subln_g.astype(jnp.float32) * (1.0 - lam_init)
    return o.reshape(B, S, ATT_WIDTH).astype(q.dtype)


def _rg_lru(xb, conv_w, conv_b, w_r, b_r, w_i, b_i, lru_lambda):
    B, S, C = xb.shape
    xc = lax.conv_general_dilated(
        xb, conv_w[:, None, :], window_strides=(1,), padding=[(CONV_WIDTH - 1, 0)],
        dimension_numbers=('NWC', 'WIO', 'NWC'), feature_group_count=C) + conv_b
    xblk = xc.reshape(B, S, LRU_BLOCKS, LRU_BLOCK_W)
    r = jax.nn.sigmoid(jnp.einsum('bsni,nio->bsno', xblk, w_r).reshape(B, S, C) + b_r)
    ig = jax.nn.sigmoid(jnp.einsum('bsni,nio->bsno', xblk, w_i).reshape(B, S, C) + b_i)
    log_a = -LRU_C * r.astype(jnp.float32) * jax.nn.softplus(-lru_lambda.astype(jnp.float32))
    a = jnp.exp(log_a)
    u = jnp.sqrt(-jnp.expm1(2.0 * log_a)) * (ig * xc).astype(jnp.float32)

    def combine(left, right):
        a_l, b_l = left
        a_r, b_r = right
        return a_r * a_l, a_r * b_l + b_r

    _, h = lax.associative_scan(combine, (a, u), axis=1)
    return h.astype(xb.dtype)


def setup_inputs(seed: int = 0) -> dict:
    key = jax.random.key(seed)
    ks = jax.random.split(key, 24)
    L, D = DEPTH, D_MODEL
    col_scale = np.concatenate([
        np.ones(ATT_QK_WIDTH * 2, np.float32),
        np.full(ATT_WIDTH, DN_BETA, np.float32),
        np.ones(ATT_WIDTH, np.float32),
        np.full(LRU_WIDTH, DN_BETA, np.float32),
        np.ones(LRU_WIDTH + 2 * D, np.float32)])
    x = jax.random.normal(ks[0], (BATCH, SEQ, D), jnp.float32)
    w_in = jax.random.normal(ks[1], (L, D, IN_TOTAL), jnp.float32) * (D ** -0.5) * jnp.asarray(col_scale)
    conv_w = jax.random.normal(ks[2], (L, CONV_WIDTH, LRU_WIDTH), jnp.float32) * (CONV_WIDTH ** -0.5)
    conv_b = 0.01 * jax.random.normal(ks[3], (L, LRU_WIDTH), jnp.float32)
    w_rgate = jax.random.normal(ks[4], (L, LRU_BLOCKS, LRU_BLOCK_W, LRU_BLOCK_W), jnp.float32) * (LRU_BLOCK_W ** -0.5)
    b_rgate = 0.01 * jax.random.normal(ks[5], (L, LRU_WIDTH), jnp.float32)
    w_igate = jax.random.normal(ks[6], (L, LRU_BLOCKS, LRU_BLOCK_W, LRU_BLOCK_W), jnp.float32) * (LRU_BLOCK_W ** -0.5)
    b_igate = 0.01 * jax.random.normal(ks[7], (L, LRU_WIDTH), jnp.float32)
    a_pow = jax.random.uniform(ks[8], (L, LRU_WIDTH), jnp.float32, 0.9, 0.999)
    a0 = a_pow ** (1.0 / LRU_C)
    lru_lambda = jnp.log(a0) - jnp.log1p(-a0)
    lam_q1 = 0.1 * jax.random.normal(ks[9], (L, ATT_HEAD_DIM), jnp.float32)
    lam_k1 = 0.1 * jax.random.normal(ks[10], (L, ATT_HEAD_DIM), jnp.float32)
    lam_q2 = 0.1 * jax.random.normal(ks[11], (L, ATT_HEAD_DIM), jnp.float32)
    lam_k2 = 0.1 * jax.random.normal(ks[12], (L, ATT_HEAD_DIM), jnp.float32)
    subln_g = 1.0 + 0.01 * jax.random.normal(ks[13], (L, ATT_V_DIM), jnp.float32)
    w_pa = jax.random.normal(ks[14], (L, ATT_WIDTH, D), jnp.float32) * (ATT_WIDTH ** -0.5) * DN_BETA
    w_pb = jax.random.normal(ks[15], (L, LRU_WIDTH, D), jnp.float32) * (LRU_WIDTH ** -0.5) * DN_BETA
    w_out = jax.random.normal(ks[16], (L, D, D), jnp.float32) * (D ** -0.5) * DN_BETA
    ln_g = 1.0 + 0.01 * jax.random.normal(ks[17], (L, D), jnp.float32)
    ln_b = 0.01 * jax.random.normal(ks[18], (L, D), jnp.float32)
    return {"x": x, "w_in": w_in, "conv_w": conv_w, "conv_b": conv_b,
            "w_rgate": w_rgate, "b_rgate": b_rgate, "w_igate": w_igate, "b_igate": b_igate,
            "lru_lambda": lru_lambda, "lam_q1": lam_q1, "lam_k1": lam_k1,
            "lam_q2": lam_q2, "lam_k2": lam_k2, "subln_g": subln_g,
            "w_pa": w_pa, "w_pb": w_pb, "w_out": w_out, "ln_g": ln_g, "ln_b": ln_b}


def reference(x, w_in, conv_w, conv_b, w_rgate, b_rgate, w_igate, b_igate, lru_lambda,
              lam_q1, lam_k1, lam_q2, lam_k2, subln_g, w_pa, w_pb, w_out, ln_g, ln_b):
    B, S, D = x.shape
    for l in range(DEPTH):
        proj = jnp.einsum('bsd,dn->bsn', x, w_in[l])
        q, k, v, g_a, x_b, g_b, m_a, m_b = jnp.split(proj, SPLIT_IDX, axis=-1)
        q = q.reshape(B, S, ATT_HEADS, 2, ATT_HEAD_DIM)
        k = k.reshape(B, S, ATT_HEADS, 2, ATT_HEAD_DIM)
        v = v.reshape(B, S, ATT_HEADS, ATT_V_DIM)
        lam_init = 0.8 - 0.6 * math.exp(-0.3 * l)
        lam = (jnp.exp(jnp.sum(lam_q1[l].astype(jnp.float32) * lam_k1[l].astype(jnp.float32)))
               - jnp.exp(jnp.sum(lam_q2[l].astype(jnp.float32) * lam_k2[l].astype(jnp.float32)))
               + lam_init)
        att = _diff_attention(q, k, v, lam, lam_init, subln_g[l]) * jax.nn.silu(g_a)
        rec = _rg_lru(x_b, conv_w[l], conv_b[l], w_rgate[l], b_rgate[l],
                      w_igate[l], b_igate[l], lru_lambda[l]) * jax.nn.silu(g_b)
        merged = (jax.nn.sigmoid(m_a) * jnp.einsum('bse,ed->bsd', att, w_pa[l])
                  + jax.nn.sigmoid(m_b) * jnp.einsum('bse,ed->bsd', rec, w_pb[l]))
        out = jnp.einsum('bsd,de->bse', merged, w_out[l])
        x = _layer_norm(DN_ALPHA * x + out, ln_g[l], ln_b[l])
    return x
```

```python
import functools
import math

import jax
import jax.numpy as jnp
from jax import lax
from jax.experimental import pallas as pl
from jax.experimental.pallas import tpu as pltpu

F32 = jnp.float32
BF16 = jnp.bfloat16

DEPTH = 2
ATT_HEADS = 8
HEAD_DIM = 64
V_DIM = 2 * HEAD_DIM
LRU_BLOCKS = 8
LRU_BLOCK_W = 128
CONV_WIDTH = 4
LRU_C = 8.0
LN_EPS = 1e-5
SUBLN_EPS = 1e-5
DN_ALPHA = (2 * DEPTH) ** 0.25

SEG_W = 1024
Q_BLK, K_BLK, V_BLK, GA_BLK, XB_BLK, GB_BLK = 0, 1, 2, 3, 4, 5
MA_BLK2, MB_BLK2 = 3, 4

VMEM_LIMIT = 56 * 1024 * 1024

INPROJ_TM, INPROJ_TN = 1024, 1024
ATT_TQ = 512
LRU_TS = 512
OUT_TM = 256


def _sigmoid(z):
    return 1.0 / (1.0 + jnp.exp(-z))


def _inproj_kernel(x_ref, w_ref, o_ref, xb_ref):
    @pl.when(pl.program_id(1) == 0)
    def _():
        xb_ref[...] = x_ref[...].astype(BF16)

    o_ref[...] = jnp.dot(xb_ref[...], w_ref[...], preferred_element_type=F32).astype(o_ref.dtype)


def _inproj(x2, w):
    m, k = x2.shape
    n = w.shape[1]
    tm, tn = INPROJ_TM, INPROJ_TN
    return pl.pallas_call(
        _inproj_kernel,
        grid=(m // tm, n // tn),
        in_specs=[pl.BlockSpec((tm, k), lambda i, j: (i, 0)),
                  pl.BlockSpec((k, tn), lambda i, j: (0, j))],
        out_specs=pl.BlockSpec((tm, tn), lambda i, j: (i, j)),
        out_shape=jax.ShapeDtypeStruct((m, n), BF16),
        scratch_shapes=[pltpu.VMEM((tm, k), BF16)],
        compiler_params=pltpu.CompilerParams(
            dimension_semantics=("parallel", "arbitrary"), vmem_limit_bytes=VMEM_LIMIT),
        name="inproj",
    )(x2, w)


def _attn_kernel(slopes_ref, q_ref, k_ref, v_ref, ga_ref, lamv_ref, g_ref, o_ref,
                 qa1_ref, ka1_ref, qa2_ref, ka2_ref, va_ref, *, lam_init, tq):
    s_len = q_ref.shape[0]
    slope = slopes_ref[pl.program_id(1)]

    lane = lax.broadcasted_iota(jnp.int32, (s_len, V_DIM), 1)
    row = lax.broadcasted_iota(jnp.int32, (s_len, V_DIM), 0)
    hi = (row >> 4).astype(F32)
    lo = (row & 15).astype(F32)
    q = q_ref[...].astype(F32) * (HEAD_DIM ** -0.5)
    k = k_ref[...].astype(F32)
    zero = jnp.zeros_like(q)
    q_aug = lambda a, b: jnp.where(lane == a, 16.0 * slope, jnp.where(lane == b, slope, zero))
    k_aug = lambda a, b: jnp.where(lane == a, hi, jnp.where(lane == b, lo, zero))
    first = lane < HEAD_DIM
    qa1_ref[...] = jnp.where(first, q, q_aug(HEAD_DIM, HEAD_DIM + 1)).astype(BF16)
    ka1_ref[...] = jnp.where(first, k, k_aug(HEAD_DIM, HEAD_DIM + 1)).astype(BF16)
    qa2_ref[...] = jnp.where(first, q_aug(0, 1), q).astype(BF16)
    ka2_ref[...] = jnp.where(first, k_aug(0, 1), k).astype(BF16)
    va_ref[:, :V_DIM] = v_ref[...]
    va_ref[:, V_DIM:] = jnp.where(lane == 0, 1.0, 0.0).astype(BF16)

    lamv = lamv_ref[...]
    lam = (jnp.exp(jnp.sum(lamv[0:1] * lamv[1:2], axis=1, keepdims=True))
           - jnp.exp(jnp.sum(lamv[2:3] * lamv[3:4], axis=1, keepdims=True)) + lam_init)

    causal = (lax.broadcasted_iota(jnp.int32, (tq, tq), 1)
              <= lax.broadcasted_iota(jnp.int32, (tq, tq), 0))
    nt = (((1,), (1,)), ((), ()))

    def one_map(qa_ref, ka_ref, r0):
        qa = qa_ref[r0:r0 + tq, :]
        s_dg = lax.dot_general(qa, ka_ref[r0:r0 + tq, :], nt, preferred_element_type=F32)
        s_dg = jnp.where(causal, s_dg, -jnp.inf)
        m = jnp.max(s_dg, axis=1, keepdims=True)
        if r0 > 0:
            s_off = lax.dot_general(qa, ka_ref[0:r0, :], nt, preferred_element_type=F32)
            m = jnp.maximum(m, jnp.max(s_off, axis=1, keepdims=True))
            p_off = jnp.exp(s_off - m).astype(BF16)
            acc = jnp.dot(p_off, va_ref[0:r0, :], preferred_element_type=F32)
        p_dg = jnp.exp(s_dg - m).astype(BF16)
        acc_dg = jnp.dot(p_dg, va_ref[r0:r0 + tq, :], preferred_element_type=F32)
        acc = acc_dg if r0 == 0 else acc + acc_dg
        return acc[:, :V_DIM] / acc[:, V_DIM:V_DIM + 1]

    for qi in range(s_len // tq):
        r0 = qi * tq
        o = one_map(qa1_ref, ka1_ref, r0) - lam * one_map(qa2_ref, ka2_ref, r0)
        o = o * lax.rsqrt(jnp.mean(o * o, axis=1, keepdims=True) + SUBLN_EPS)
        o = o * g_ref[...] * (1.0 - lam_init)
        ga = ga_ref[r0:r0 + tq, :].astype(F32)
        o_ref[r0:r0 + tq, :] = (o * (ga * _sigmoid(ga))).astype(o_ref.dtype)


def _attention(proj3, slopes, lamv, subln_g, lam_init):
    b, s, _ = proj3.shape
    per_head = SEG_W // V_DIM
    blk = lambda seg: pl.BlockSpec((None, s, V_DIM), lambda bi, hi, seg=seg: (bi, 0, seg * per_head + hi))
    kern = functools.partial(_attn_kernel, lam_init=lam_init, tq=ATT_TQ)
    return pl.pallas_call(
        kern,
        grid=(b, ATT_HEADS),
        in_specs=[pl.BlockSpec(memory_space=pltpu.SMEM),
                  blk(Q_BLK), blk(K_BLK), blk(V_BLK), blk(GA_BLK),
                  pl.BlockSpec((4, HEAD_DIM), lambda bi, hi: (0, 0)),
                  pl.BlockSpec((1, V_DIM), lambda bi, hi: (0, 0))],
        out_specs=pl.BlockSpec((None, s, V_DIM), lambda bi, hi: (bi, 0, hi)),
        out_shape=jax.ShapeDtypeStruct((b, s, ATT_HEADS * V_DIM), BF16),
        scratch_shapes=[pltpu.VMEM((s, V_DIM), BF16)] * 4 + [pltpu.VMEM((s, 2 * V_DIM), BF16)],
        compiler_params=pltpu.CompilerParams(
            dimension_semantics=("parallel", "parallel"), vmem_limit_bytes=VMEM_LIMIT),
        name="diff_attention",
    )(slopes, proj3, proj3, proj3, proj3, lamv, subln_g)


def _lru_kernel(xb_ref, gb_ref, cw_ref, cb_ref, wr_ref, br_ref, wi_ref, bi_ref, lam_ref, o_ref,
                xpad_ref, a_ref, u_ref, h_ref):
    ts, c = xb_ref.shape
    halo = 8

    @pl.when(pl.program_id(1) == 0)
    def _():
        xpad_ref[0:halo, :] = jnp.zeros((halo, c), F32)
        h_ref[...] = jnp.zeros_like(h_ref)

    xpad_ref[halo:, :] = xb_ref[...].astype(F32)
    xc = cb_ref[...]
    for j in range(CONV_WIDTH):
        off = halo - (CONV_WIDTH - 1) + j
        xc = xc + cw_ref[j:j + 1, :] * xpad_ref[off:off + ts, :]
    xpad_ref[0:halo, :] = xpad_ref[ts:ts + halo, :]

    z = -lam_ref[...]
    softplus = jnp.maximum(z, 0.0) + jnp.log1p(jnp.exp(-jnp.abs(z)))
    for n in range(LRU_BLOCKS):
        sl = slice(n * LRU_BLOCK_W, (n + 1) * LRU_BLOCK_W)
        xn = xc[:, sl]
        xn16 = xn.astype(BF16)
        r = _sigmoid(jnp.dot(xn16, wr_ref[n], preferred_element_type=F32) + br_ref[:, sl])
        ig = _sigmoid(jnp.dot(xn16, wi_ref[n], preferred_element_type=F32) + bi_ref[:, sl])
        log_a = -LRU_C * r * softplus[:, sl]
        a = jnp.exp(log_a)
        a_ref[:, sl] = a
        u_ref[:, sl] = jnp.sqrt(1.0 - a * a) * (ig * xn)

    def tile(t8, h):
        base = pl.multiple_of(t8 * 8, 8)
        for r8 in range(8):
            h = a_ref[pl.ds(base + r8, 1), :] * h + u_ref[pl.ds(base + r8, 1), :]
            u_ref[pl.ds(base + r8, 1), :] = h
        return h

    h_ref[...] = lax.fori_loop(0, ts // 8, tile, h_ref[...])
    gb = gb_ref[...].astype(F32)
    o_ref[...] = (u_ref[...] * (gb * _sigmoid(gb))).astype(o_ref.dtype)


def _lru(proj3, conv_w, conv_b, w_r, b_r, w_i, b_i, lru_lambda):
    b, s, _ = proj3.shape
    c = SEG_W
    ts = LRU_TS
    row = lambda a: a.reshape(1, c)
    const = lambda shape: pl.BlockSpec(shape, lambda bi, si: (0,) * len(shape))
    return pl.pallas_call(
        _lru_kernel,
        grid=(b, s // ts),
        in_specs=[pl.BlockSpec((None, ts, c), lambda bi, si: (bi, si, XB_BLK)),
                  pl.BlockSpec((None, ts, c), lambda bi, si: (bi, si, GB_BLK)),
                  const((CONV_WIDTH, c)), const((1, c)),
                  const((LRU_BLOCKS, LRU_BLOCK_W, LRU_BLOCK_W)), const((1, c)),
                  const((LRU_BLOCKS, LRU_BLOCK_W, LRU_BLOCK_W)), const((1, c)),
                  const((1, c))],
        out_specs=pl.BlockSpec((None, ts, c), lambda bi, si: (bi, si, 0)),
        out_shape=jax.ShapeDtypeStruct((b, s, c), BF16),
        scratch_shapes=[pltpu.VMEM((ts + 8, c), F32), pltpu.VMEM((ts, c), F32),
                        pltpu.VMEM((ts, c), F32), pltpu.VMEM((1, c), F32)],
        compiler_params=pltpu.CompilerParams(
            dimension_semantics=("parallel", "arbitrary"), vmem_limit_bytes=VMEM_LIMIT),
        name="rg_lru",
    )(proj3, proj3, conv_w, row(conv_b), w_r.astype(BF16), row(b_r), w_i.astype(BF16), row(b_i),
      row(lru_lambda))


def _outproj_kernel(att_ref, rec_ref, ma_ref, mb_ref, x_ref, wpa_ref, wpb_ref, wout_ref,
                    g_ref, b_ref, o_ref):
    pa = jnp.dot(att_ref[...], wpa_ref[...], preferred_element_type=F32)
    pb = jnp.dot(rec_ref[...], wpb_ref[...], preferred_element_type=F32)
    merged = (_sigmoid(ma_ref[...].astype(F32)) * pa + _sigmoid(mb_ref[...].astype(F32)) * pb)
    out = jnp.dot(merged.astype(BF16), wout_ref[...], preferred_element_type=F32)
    y = DN_ALPHA * x_ref[...] + out
    mu = jnp.mean(y, axis=1, keepdims=True)
    yc = y - mu
    var = jnp.mean(yc * yc, axis=1, keepdims=True)
    o_ref[...] = yc * lax.rsqrt(var + LN_EPS) * g_ref[...] + b_ref[...]


def _outproj(att2, rec2, proj2, x2, w_pa, w_pb, w_out, ln_g, ln_b):
    m, d = x2.shape
    e = att2.shape[1]
    tm = OUT_TM
    resident = lambda shape: pl.BlockSpec(shape, lambda i: (0,) * len(shape), pipeline_mode=pl.Buffered(1))
    return pl.pallas_call(
        _outproj_kernel,
        grid=(m // tm,),
        in_specs=[pl.BlockSpec((tm, e), lambda i: (i, 0)),
                  pl.BlockSpec((tm, e), lambda i: (i, 0)),
                  pl.BlockSpec((tm, d), lambda i: (i, MA_BLK2)),
                  pl.BlockSpec((tm, d), lambda i: (i, MB_BLK2)),
                  pl.BlockSpec((tm, d), lambda i: (i, 0)),
                  resident((e, d)), resident((e, d)), resident((d, d)),
                  resident((1, d)), resident((1, d))],
        out_specs=pl.BlockSpec((tm, d), lambda i: (i, 0)),
        out_shape=jax.ShapeDtypeStruct((m, d), F32),
        compiler_params=pltpu.CompilerParams(
            dimension_semantics=("parallel",), vmem_limit_bytes=VMEM_LIMIT),
        name="outproj_ln",
    )(att2, rec2, proj2, proj2, x2, w_pa, w_pb, w_out, ln_g.reshape(1, d), ln_b.reshape(1, d))


def kernel(x, w_in, conv_w, conv_b, w_rgate, b_rgate, w_igate, b_igate, lru_lambda,
           lam_q1, lam_k1, lam_q2, lam_k2, subln_g, w_pa, w_pb, w_out, ln_g, ln_b):
    b, s, d = x.shape
    m = b * s
    slopes = jnp.exp2(-(8.0 / ATT_HEADS) * jnp.arange(1, ATT_HEADS + 1, dtype=F32))
    x2 = x.reshape(m, d)
    for l in range(DEPTH):
        lam_init = 0.8 - 0.6 * math.exp(-0.3 * l)
        proj2 = _inproj(x2, w_in[l].astype(BF16))
        n = proj2.shape[1]
        proj3 = proj2.reshape(b, s, n)
        lamv = jnp.stack([lam_q1[l], lam_k1[l], lam_q2[l], lam_k2[l]]).astype(F32)
        att = _attention(proj3, slopes, lamv, subln_g[l].reshape(1, V_DIM).astype(F32), lam_init)
        rec = _lru(proj3, conv_w[l], conv_b[l], w_rgate[l], b_rgate[l], w_igate[l], b_igate[l],
                   lru_lambda[l])
        x2 = _outproj(att.reshape(m, -1), rec.reshape(m, -1), proj2, x2,
                      w_pa[l].astype(BF16), w_pb[l].astype(BF16), w_out[l].astype(BF16),
                      ln_g[l], ln_b[l])
    return x2.reshape(b, s, d)
```

```python
import functools
import math

import jax
import jax.numpy as jnp
from jax import lax
from jax.experimental import pallas as pl
from jax.experimental.pallas import tpu as pltpu

F32 = jnp.float32
BF16 = jnp.bfloat16

DEPTH = 2
ATT_HEADS = 8
HEAD_DIM = 64
V_DIM = 2 * HEAD_DIM
LRU_BLOCKS = 8
LRU_BLOCK_W = 128
CONV_WIDTH = 4
LRU_C = 8.0
LN_EPS = 1e-5
SUBLN_EPS = 1e-5
DN_ALPHA = (2 * DEPTH) ** 0.25
LOG2E = math.log2(math.e)
KPOS_RADIX = 16
VT_PAD = 16

SEG_W = 1024
Q_BLK, K_BLK, V_BLK, GA_BLK, XB_BLK, GB_BLK = 0, 1, 2, 3, 4, 5
MA_BLK2, MB_BLK2 = 3, 4

VMEM_LIMIT = 56 * 1024 * 1024

INPROJ_TM, INPROJ_TN = 1024, 1024
ATT_TQ = 512
ATT_LOOKAHEAD = 2
LRU_TS = 512
LRU_SCAN_UNROLL = 2
OUT_TM = 256


def _sigmoid(z):
    return 0.5 * jnp.tanh(0.5 * z) + 0.5


def _silu(z):
    hz = 0.5 * z
    return hz * jnp.tanh(hz) + hz


def _inproj_kernel(x_ref, w_ref, o_ref, xb_ref):
    @pl.when(pl.program_id(1) == 0)
    def _():
        xb_ref[...] = x_ref[...].astype(BF16)

    o_ref[...] = jnp.dot(xb_ref[...], w_ref[...], preferred_element_type=F32).astype(o_ref.dtype)


def _inproj(x2, w):
    m, k = x2.shape
    n = w.shape[1]
    tm, tn = INPROJ_TM, INPROJ_TN
    return pl.pallas_call(
        _inproj_kernel,
        grid=(m // tm, n // tn),
        in_specs=[pl.BlockSpec((tm, k), lambda i, j: (i, 0)),
                  pl.BlockSpec((k, tn), lambda i, j: (0, j))],
        out_specs=pl.BlockSpec((tm, tn), lambda i, j: (i, j)),
        out_shape=jax.ShapeDtypeStruct((m, n), BF16),
        scratch_shapes=[pltpu.VMEM((tm, k), BF16)],
        compiler_params=pltpu.CompilerParams(
            dimension_semantics=("parallel", "arbitrary"), vmem_limit_bytes=VMEM_LIMIT),
        name="inproj",
    )(x2, w)


def _attn_kernel(slopes_ref, q_ref, k_ref, v_ref, ga_ref, kpos1_ref, kpos2_ref, lamv_ref, g_ref, o_ref,
                 qa1_ref, ka1_ref, qa2_ref, ka2_ref, vt_ref, s_ref, *, lam_init, tq):
    s_len = q_ref.shape[0]
    slope2 = jnp.full((1, V_DIM), slopes_ref[pl.program_id(1)] * LOG2E, F32)
    s1 = slope2.astype(BF16).astype(F32)
    s2 = (slope2 - s1).astype(BF16).astype(F32)
    s3 = slope2 - s1 - s2
    lane_row = lax.broadcasted_iota(jnp.int32, (1, V_DIM), 1)

    def slope_lanes(base):
        r = jnp.zeros((1, V_DIM), F32)
        for idx, piece in enumerate((s1, s2, s3)):
            r = jnp.where(lane_row == base + 2 * idx, float(KPOS_RADIX) * piece, r)
            r = jnp.where(lane_row == base + 2 * idx + 1, piece, r)
        return r

    first = lax.broadcasted_iota(jnp.int32, (s_len, V_DIM), 1) < HEAD_DIM
    q = q_ref[...].astype(F32) * (HEAD_DIM ** -0.5 * LOG2E)
    qa1_ref[...] = jnp.where(first, q, slope_lanes(HEAD_DIM)).astype(BF16)
    qa2_ref[...] = jnp.where(first, slope_lanes(0), q).astype(BF16)
    k = k_ref[...]
    ka1_ref[...] = jnp.where(first, k, kpos1_ref[...])
    ka2_ref[...] = jnp.where(first, kpos2_ref[...], k)
    vt_ref[0:V_DIM, :] = v_ref[...].astype(F32).T.astype(BF16)
    ones_row = lax.broadcasted_iota(jnp.int32, (VT_PAD, s_len), 0) == 0
    vt_ref[V_DIM:, :] = jnp.where(ones_row, 1.0, 0.0).astype(BF16)

    lamv = lamv_ref[...]
    lam = (jnp.exp(jnp.sum(lamv[0:1] * lamv[1:2], axis=1, keepdims=True))
           - jnp.exp(jnp.sum(lamv[2:3] * lamv[3:4], axis=1, keepdims=True)) + lam_init)

    causal = (lax.broadcasted_iota(jnp.int32, (tq, tq), 0)
              <= lax.broadcasted_iota(jnp.int32, (tq, tq), 1))
    nt = (((1,), (1,)), ((), ()))

    def scores(qa_ref, ka_ref, r0, slot):
        qa = qa_ref[r0:r0 + tq, :]
        m = None
        for c0 in range(0, r0 + tq, tq):
            s = lax.dot_general(ka_ref[c0:c0 + tq, :], qa, nt, preferred_element_type=F32)
            if c0 == r0:
                s = jnp.where(causal, s, -jnp.inf)
            s_ref[slot, c0:c0 + tq, :] = s
            cm = jnp.max(s, axis=0, keepdims=True)
            m = cm if m is None else jnp.maximum(m, cm)
        return m

    def weighted_values(m, r0, slot):
        acc = None
        for c0 in range(0, r0 + tq, tq):
            p = jnp.exp2(s_ref[slot, c0:c0 + tq, :] - m).astype(BF16)
            part = jnp.dot(vt_ref[:, c0:c0 + tq], p, preferred_element_type=F32)
            acc = part if acc is None else acc + part
        return acc[0:V_DIM] / acc[V_DIM:V_DIM + 1]

    units = [(qi * tq, mp) for qi in range(s_len // tq) for mp in range(2)]
    refs = ((qa1_ref, ka1_ref), (qa2_ref, ka2_ref))
    gain = g_ref[...] * (1.0 - lam_init)
    n_slots = ATT_LOOKAHEAD + 1
    pending = [scores(*refs[mp], r0, i) for i, (r0, mp) in enumerate(units[:ATT_LOOKAHEAD])]
    first_map = None
    for idx, (r0, mp) in enumerate(units):
        ahead = idx + ATT_LOOKAHEAD
        if ahead < len(units):
            nr0, nmp = units[ahead]
            pending.append(scores(*refs[nmp], nr0, ahead % n_slots))
        o_map = weighted_values(pending.pop(0), r0, idx % n_slots)
        if mp == 0:
            first_map = o_map
            continue
        ot = first_map - lam * o_map
        ot = ot * lax.rsqrt(jnp.mean(ot * ot, axis=0, keepdims=True) + SUBLN_EPS)
        ga = ga_ref[r0:r0 + tq, :].astype(F32)
        o_ref[r0:r0 + tq, :] = (ot.T * gain * _silu(ga)).astype(o_ref.dtype)


def _kpos_lanes(s_len, base):
    pos = jnp.arange(s_len, dtype=jnp.int32)[:, None]
    lane = jnp.arange(V_DIM, dtype=jnp.int32)[None, :] - base
    digit = jnp.where(lane % 2 == 0, pos // KPOS_RADIX, pos % KPOS_RADIX)
    return jnp.where((lane >= 0) & (lane < 6), digit, 0).astype(BF16)


def _attention(proj3, slopes, lamv, subln_g, lam_init):
    b, s, _ = proj3.shape
    per_head = SEG_W // V_DIM
    blk = lambda seg: pl.BlockSpec((None, s, V_DIM), lambda bi, hi, seg=seg: (bi, 0, seg * per_head + hi))
    const = lambda shape: pl.BlockSpec(shape, lambda bi, hi: (0,) * len(shape))
    kern = functools.partial(_attn_kernel, lam_init=lam_init, tq=ATT_TQ)
    return pl.pallas_call(
        kern,
        grid=(b, ATT_HEADS),
        in_specs=[pl.BlockSpec(memory_space=pltpu.SMEM),
                  blk(Q_BLK), blk(K_BLK), blk(V_BLK), blk(GA_BLK),
                  const((s, V_DIM)), const((s, V_DIM)), const((4, HEAD_DIM)), const((1, V_DIM))],
        out_specs=pl.BlockSpec((None, s, V_DIM), lambda bi, hi: (bi, 0, hi)),
        out_shape=jax.ShapeDtypeStruct((b, s, ATT_HEADS * V_DIM), BF16),
        scratch_shapes=[pltpu.VMEM((s, V_DIM), BF16)] * 4 + [pltpu.VMEM((V_DIM + VT_PAD, s), BF16),
                                                             pltpu.VMEM((ATT_LOOKAHEAD + 1, s, ATT_TQ), F32)],
        compiler_params=pltpu.CompilerParams(
            dimension_semantics=("parallel", "parallel"), vmem_limit_bytes=VMEM_LIMIT),
        name="diff_attention",
    )(slopes, proj3, proj3, proj3, proj3, _kpos_lanes(s, HEAD_DIM), _kpos_lanes(s, 0), lamv, subln_g)


def _lru_kernel(xb_ref, gb_ref, cw_ref, cb_ref, wr_ref, br_ref, wi_ref, bi_ref, lam_ref, o_ref,
                xpad_ref, a_ref, u_ref, hs_ref, h_ref):
    ts, c = xb_ref.shape
    halo = 8

    @pl.when(pl.program_id(1) == 0)
    def _():
        xpad_ref[0:halo, :] = jnp.zeros((halo, c), F32)
        h_ref[...] = jnp.zeros_like(h_ref)

    xpad_ref[halo:, :] = xb_ref[...].astype(F32)
    xp = xpad_ref[...]
    xc = cb_ref[...] + cw_ref[CONV_WIDTH - 1:CONV_WIDTH, :] * xp[halo:, :]
    for d in range(1, CONV_WIDTH):
        xc = xc + cw_ref[CONV_WIDTH - 1 - d:CONV_WIDTH - d, :] * pltpu.roll(xp, d, axis=0)[halo:, :]
    xpad_ref[0:halo, :] = xpad_ref[ts:ts + halo, :]

    z = -lam_ref[...]
    softplus = jnp.maximum(z, 0.0) + jnp.log1p(jnp.exp(-jnp.abs(z)))
    decay2 = (-LRU_C * LOG2E) * softplus
    for n in range(LRU_BLOCKS):
        sl = slice(n * LRU_BLOCK_W, (n + 1) * LRU_BLOCK_W)
        xn = xc[:, sl]
        xn16 = xn.astype(BF16)
        r = _sigmoid(jnp.dot(xn16, wr_ref[n], preferred_element_type=F32) + br_ref[:, sl])
        ig = _sigmoid(jnp.dot(xn16, wi_ref[n], preferred_element_type=F32) + bi_ref[:, sl])
        a = jnp.exp2(r * decay2[:, sl])
        a_ref[:, sl] = a
        v = 1.0 - a * a
        u_ref[:, sl] = jnp.where(v > 0.0, v * lax.rsqrt(v), 0.0) * (ig * xn)

    row = lax.broadcasted_iota(jnp.int32, (8, c), 0)
    keep = [row >= d for d in (1, 2, 4)]

    def tile(t8, h):
        base = pl.multiple_of(t8 * 8, 8)
        a = a_ref[pl.ds(base, 8), :]
        u = u_ref[pl.ds(base, 8), :]
        for d, k in zip((1, 2, 4), keep):
            u = a * jnp.where(k, pltpu.roll(u, d, axis=0), 0.0) + u
            a = a * jnp.where(k, pltpu.roll(a, d, axis=0), 1.0)
        hs = a * h + u
        hs_ref[pl.ds(base, 8), :] = hs
        return hs[7:8, :]

    h_ref[...] = lax.fori_loop(0, ts // 8, tile, h_ref[...], unroll=LRU_SCAN_UNROLL)
    o_ref[...] = (hs_ref[...] * _silu(gb_ref[...].astype(F32))).astype(o_ref.dtype)


def _lru(proj3, conv_w, conv_b, w_r, b_r, w_i, b_i, lru_lambda):
    b, s, _ = proj3.shape
    c = SEG_W
    ts = LRU_TS
    row = lambda a: a.reshape(1, c)
    const = lambda shape: pl.BlockSpec(shape, lambda bi, si: (0,) * len(shape))
    return pl.pallas_call(
        _lru_kernel,
        grid=(b, s // ts),
        in_specs=[pl.BlockSpec((None, ts, c), lambda bi, si: (bi, si, XB_BLK)),
                  pl.BlockSpec((None, ts, c), lambda bi, si: (bi, si, GB_BLK)),
                  const((CONV_WIDTH, c)), const((1, c)),
                  const((LRU_BLOCKS, LRU_BLOCK_W, LRU_BLOCK_W)), const((1, c)),
                  const((LRU_BLOCKS, LRU_BLOCK_W, LRU_BLOCK_W)), const((1, c)),
                  const((1, c))],
        out_specs=pl.BlockSpec((None, ts, c), lambda bi, si: (bi, si, 0)),
        out_shape=jax.ShapeDtypeStruct((b, s, c), BF16),
        scratch_shapes=[pltpu.VMEM((ts + 8, c), F32), pltpu.VMEM((ts, c), F32),
                        pltpu.VMEM((ts, c), F32), pltpu.VMEM((ts, c), F32), pltpu.VMEM((1, c), F32)],
        compiler_params=pltpu.CompilerParams(
            dimension_semantics=("parallel", "arbitrary"), vmem_limit_bytes=VMEM_LIMIT),
        name="rg_lru",
    )(proj3, proj3, conv_w, row(conv_b), w_r.astype(BF16), row(b_r), w_i.astype(BF16), row(b_i),
      row(lru_lambda))


def _outproj_kernel(att_ref, rec_ref, ma_ref, mb_ref, x_ref, wpa_ref, wpb_ref, wout_ref,
                    g_ref, b_ref, o_ref):
    pa = jnp.dot(att_ref[...], wpa_ref[...], preferred_element_type=F32)
    pb = jnp.dot(rec_ref[...], wpb_ref[...], preferred_element_type=F32)
    merged = (_sigmoid(ma_ref[...].astype(F32)) * pa + _sigmoid(mb_ref[...].astype(F32)) * pb)
    out = jnp.dot(merged.astype(BF16), wout_ref[...], preferred_element_type=F32)
    y = DN_ALPHA * x_ref[...] + out
    mu = jnp.mean(y, axis=1, keepdims=True)
    yc = y - mu
    var = jnp.mean(yc * yc, axis=1, keepdims=True)
    o_ref[...] = yc * lax.rsqrt(var + LN_EPS) * g_ref[...] + b_ref[...]


def _outproj(att2, rec2, proj2, x2, w_pa, w_pb, w_out, ln_g, ln_b):
    m, d = x2.shape
    e = att2.shape[1]
    tm = OUT_TM
    resident = lambda shape: pl.BlockSpec(shape, lambda i: (0,) * len(shape), pipeline_mode=pl.Buffered(1))
    return pl.pallas_call(
        _outproj_kernel,
        grid=(m // tm,),
        in_specs=[pl.BlockSpec((tm, e), lambda i: (i, 0)),
                  pl.BlockSpec((tm, e), lambda i: (i, 0)),
                  pl.BlockSpec((tm, d), lambda i: (i, MA_BLK2)),
                  pl.BlockSpec((tm, d), lambda i: (i, MB_BLK2)),
                  pl.BlockSpec((tm, d), lambda i: (i, 0)),
                  resident((e, d)), resident((e, d)), resident((d, d)),
                  resident((1, d)), resident((1, d))],
        out_specs=pl.BlockSpec((tm, d), lambda i: (i, 0)),
        out_shape=jax.ShapeDtypeStruct((m, d), F32),
        compiler_params=pltpu.CompilerParams(
            dimension_semantics=("parallel",), vmem_limit_bytes=VMEM_LIMIT),
        name="outproj_ln",
    )(att2, rec2, proj2, proj2, x2, w_pa, w_pb, w_out, ln_g.reshape(1, d), ln_b.reshape(1, d))


def kernel(x, w_in, conv_w, conv_b, w_rgate, b_rgate, w_igate, b_igate, lru_lambda,
           lam_q1, lam_k1, lam_q2, lam_k2, subln_g, w_pa, w_pb, w_out, ln_g, ln_b):
    b, s, d = x.shape
    m = b * s
    slopes = jnp.exp2(-(8.0 / ATT_HEADS) * jnp.arange(1, ATT_HEADS + 1, dtype=F32))
    x2 = x.reshape(m, d)
    for l in range(DEPTH):
        lam_init = 0.8 - 0.6 * math.exp(-0.3 * l)
        proj2 = _inproj(x2, w_in[l].astype(BF16))
        n = proj2.shape[1]
        proj3 = proj2.reshape(b, s, n)
        lamv = jnp.stack([lam_q1[l], lam_k1[l], lam_q2[l], lam_k2[l]]).astype(F32)
        att = _attention(proj3, slopes, lamv, subln_g[l].reshape(1, V_DIM).astype(F32), lam_init)
        rec = _lru(proj3, conv_w[l], conv_b[l], w_rgate[l], b_rgate[l], w_igate[l], b_igate[l],
                   lru_lambda[l])
        x2 = _outproj(att.reshape(m, -1), rec.reshape(m, -1), proj2, x2,
                      w_pa[l].astype(BF16), w_pb[l].astype(BF16), w_out[l].astype(BF16),
                      ln_g[l], ln_b[l])
    return x2.reshape(b, s, d)
```

```python
import functools
import math

import jax
import jax.numpy as jnp
from jax import lax
from jax.experimental import pallas as pl
from jax.experimental.pallas import tpu as pltpu

F32 = jnp.float32
BF16 = jnp.bfloat16

DEPTH = 2
ATT_HEADS = 8
HEAD_DIM = 64
V_DIM = 2 * HEAD_DIM
LRU_BLOCKS = 8
LRU_BLOCK_W = 128
CONV_WIDTH = 4
LRU_C = 8.0
LN_EPS = 1e-5
SUBLN_EPS = 1e-5
DN_ALPHA = (2 * DEPTH) ** 0.25
LOG2E = math.log2(math.e)
KPOS_RADIX = 16
VT_PAD = 16

SEG_W = 1024
Q_BLK, K_BLK, V_BLK, GA_BLK, XB_BLK, GB_BLK = 0, 1, 2, 3, 4, 5
MERGE_COL0 = 6 * SEG_W

FP8 = jnp.float8_e4m3fn
FP8_MAX = 448.0
QUANT_TN = 512

VMEM_LIMIT = 56 * 1024 * 1024

INPROJ_TM, INPROJ_TN = 1024, 1024
ATT_TQ = 512
ATT_LOOKAHEAD = 2
LRU_TS = 512
LRU_SCAN_UNROLL = 2
OUT_TM = 512
OUT_SUB = 256


def _sigmoid(z):
    return 0.5 * jnp.tanh(0.5 * z) + 0.5


def _silu(z):
    hz = 0.5 * z
    return hz * jnp.tanh(hz) + hz


def _inproj_kernel(x_ref, w_ref, o_ref, xb_ref):
    @pl.when(pl.program_id(1) == 0)
    def _():
        xb_ref[...] = x_ref[...].astype(BF16)

    o_ref[...] = jnp.dot(xb_ref[...], w_ref[...], preferred_element_type=F32).astype(o_ref.dtype)


def _inproj(x2, w):
    m, k = x2.shape
    n = w.shape[1]
    tm, tn = INPROJ_TM, INPROJ_TN
    return pl.pallas_call(
        _inproj_kernel,
        grid=(m // tm, n // tn),
        in_specs=[pl.BlockSpec((tm, k), lambda i, j: (i, 0)),
                  pl.BlockSpec((k, tn), lambda i, j: (0, j))],
        out_specs=pl.BlockSpec((tm, tn), lambda i, j: (i, j)),
        out_shape=jax.ShapeDtypeStruct((m, n), BF16),
        scratch_shapes=[pltpu.VMEM((tm, k), BF16)],
        compiler_params=pltpu.CompilerParams(
            dimension_semantics=("parallel", "arbitrary"), vmem_limit_bytes=VMEM_LIMIT),
        name="inproj",
    )(x2, w)


def _quantize_cols_kernel(w_ref, q_ref, s_ref):
    w = w_ref[...]
    amax = jnp.max(jnp.abs(w), axis=0, keepdims=True)
    scale = jnp.where(amax > 0.0, amax * (1.0 / FP8_MAX), 1.0)
    s_ref[...] = scale
    q_ref[...] = (w * (1.0 / scale)).astype(FP8)


def _quantize_cols(w_in_l, col0):
    k, n_all = w_in_l.shape
    tn = QUANT_TN
    blk0 = col0 // tn
    n = n_all - col0
    return pl.pallas_call(
        _quantize_cols_kernel,
        grid=(n // tn,),
        in_specs=[pl.BlockSpec((k, tn), lambda j: (0, blk0 + j))],
        out_specs=[pl.BlockSpec((k, tn), lambda j: (0, j)), pl.BlockSpec((1, tn), lambda j: (0, j))],
        out_shape=[jax.ShapeDtypeStruct((k, n), FP8), jax.ShapeDtypeStruct((1, n), F32)],
        compiler_params=pltpu.CompilerParams(
            dimension_semantics=("parallel",), vmem_limit_bytes=VMEM_LIMIT),
        name="quantize_cols",
    )(w_in_l)


def _inproj8_kernel(x_ref, w_ref, sw_ref, o_ref, x8_ref, sx_ref):
    @pl.when(pl.program_id(1) == 0)
    def _():
        x = x_ref[...]
        amax = jnp.max(jnp.abs(x), axis=1, keepdims=True)
        scale = jnp.where(amax > 0.0, amax * (1.0 / FP8_MAX), 1.0)
        sx_ref[...] = scale
        x8_ref[...] = (x * (1.0 / scale)).astype(FP8)

    acc = jnp.dot(x8_ref[...], w_ref[...], preferred_element_type=F32)
    o_ref[...] = (acc * sx_ref[...] * sw_ref[...]).astype(o_ref.dtype)


def _inproj8(x2, w8, sw):
    m, k = x2.shape
    n = w8.shape[1]
    tm, tn = INPROJ_TM, INPROJ_TN
    return pl.pallas_call(
        _inproj8_kernel,
        grid=(m // tm, n // tn),
        in_specs=[pl.BlockSpec((tm, k), lambda i, j: (i, 0)),
                  pl.BlockSpec((k, tn), lambda i, j: (0, j)),
                  pl.BlockSpec((1, tn), lambda i, j: (0, j))],
        out_specs=pl.BlockSpec((tm, tn), lambda i, j: (i, j)),
        out_shape=jax.ShapeDtypeStruct((m, n), BF16),
        scratch_shapes=[pltpu.VMEM((tm, k), FP8), pltpu.VMEM((tm, 1), F32)],
        compiler_params=pltpu.CompilerParams(
            dimension_semantics=("parallel", "arbitrary"), vmem_limit_bytes=VMEM_LIMIT),
        name="inproj_fp8",
    )(x2, w8, sw)


def _attn_kernel(slopes_ref, q_ref, k_ref, v_ref, ga_ref, kpos1_ref, kpos2_ref, lamv_ref, g_ref, o_ref,
                 qa1_ref, ka1_ref, qa2_ref, ka2_ref, vt_ref, s_ref, *, lam_init, tq):
    s_len = q_ref.shape[0]
    slope2 = jnp.full((1, V_DIM), slopes_ref[pl.program_id(1)] * LOG2E, F32)
    s1 = slope2.astype(BF16).astype(F32)
    s2 = (slope2 - s1).astype(BF16).astype(F32)
    s3 = slope2 - s1 - s2
    lane_row = lax.broadcasted_iota(jnp.int32, (1, V_DIM), 1)

    def slope_lanes(base):
        r = jnp.zeros((1, V_DIM), F32)
        for idx, piece in enumerate((s1, s2, s3)):
            r = jnp.where(lane_row == base + 2 * idx, float(KPOS_RADIX) * piece, r)
            r = jnp.where(lane_row == base + 2 * idx + 1, piece, r)
        return r

    first = lax.broadcasted_iota(jnp.int32, (s_len, V_DIM), 1) < HEAD_DIM
    q = q_ref[...].astype(F32) * (HEAD_DIM ** -0.5 * LOG2E)
    qa1_ref[...] = jnp.where(first, q, slope_lanes(HEAD_DIM)).astype(BF16)
    qa2_ref[...] = jnp.where(first, slope_lanes(0), q).astype(BF16)
    k = k_ref[...]
    ka1_ref[...] = jnp.where(first, k, kpos1_ref[...])
    ka2_ref[...] = jnp.where(first, kpos2_ref[...], k)
    vt_ref[0:V_DIM, :] = v_ref[...].astype(F32).T.astype(BF16)
    ones_row = lax.broadcasted_iota(jnp.int32, (VT_PAD, s_len), 0) == 0
    vt_ref[V_DIM:, :] = jnp.where(ones_row, 1.0, 0.0).astype(BF16)

    lamv = lamv_ref[...]
    lam = (jnp.exp(jnp.sum(lamv[0:1] * lamv[1:2], axis=1, keepdims=True))
           - jnp.exp(jnp.sum(lamv[2:3] * lamv[3:4], axis=1, keepdims=True)) + lam_init)

    causal = (lax.broadcasted_iota(jnp.int32, (tq, tq), 0)
              <= lax.broadcasted_iota(jnp.int32, (tq, tq), 1))
    nt = (((1,), (1,)), ((), ()))

    def scores(qa_ref, ka_ref, r0, slot):
        qa = qa_ref[r0:r0 + tq, :]
        m = None
        for c0 in range(0, r0 + tq, tq):
            s = lax.dot_general(ka_ref[c0:c0 + tq, :], qa, nt, preferred_element_type=F32)
            if c0 == r0:
                s = jnp.where(causal, s, -jnp.inf)
            s_ref[slot, c0:c0 + tq, :] = s
            cm = jnp.max(s, axis=0, keepdims=True)
            m = cm if m is None else jnp.maximum(m, cm)
        return m

    def weighted_values(m, r0, slot):
        acc = None
        for c0 in range(0, r0 + tq, tq):
            p = jnp.exp2(s_ref[slot, c0:c0 + tq, :] - m).astype(BF16)
            part = jnp.dot(vt_ref[:, c0:c0 + tq], p, preferred_element_type=F32)
            acc = part if acc is None else acc + part
        return acc[0:V_DIM] / acc[V_DIM:V_DIM + 1]

    units = [(qi * tq, mp) for qi in range(s_len // tq) for mp in range(2)]
    refs = ((qa1_ref, ka1_ref), (qa2_ref, ka2_ref))
    gain = g_ref[...] * (1.0 - lam_init)
    n_slots = ATT_LOOKAHEAD + 1
    pending = [scores(*refs[mp], r0, i) for i, (r0, mp) in enumerate(units[:ATT_LOOKAHEAD])]
    first_map = None
    for idx, (r0, mp) in enumerate(units):
        ahead = idx + ATT_LOOKAHEAD
        if ahead < len(units):
            nr0, nmp = units[ahead]
            pending.append(scores(*refs[nmp], nr0, ahead % n_slots))
        o_map = weighted_values(pending.pop(0), r0, idx % n_slots)
        if mp == 0:
            first_map = o_map
            continue
        ot = first_map - lam * o_map
        ot = ot * lax.rsqrt(jnp.mean(ot * ot, axis=0, keepdims=True) + SUBLN_EPS)
        ga = ga_ref[r0:r0 + tq, :].astype(F32)
        o_ref[r0:r0 + tq, :] = (ot.T * gain * _silu(ga)).astype(o_ref.dtype)


def _kpos_lanes(s_len, base):
    pos = jnp.arange(s_len, dtype=jnp.int32)[:, None]
    lane = jnp.arange(V_DIM, dtype=jnp.int32)[None, :] - base
    digit = jnp.where(lane % 2 == 0, pos // KPOS_RADIX, pos % KPOS_RADIX)
    return jnp.where((lane >= 0) & (lane < 6), digit, 0).astype(BF16)


def _attention(proj3, slopes, lamv, subln_g, lam_init):
    b, s, _ = proj3.shape
    per_head = SEG_W // V_DIM
    blk = lambda seg: pl.BlockSpec((None, s, V_DIM), lambda bi, hi, seg=seg: (bi, 0, seg * per_head + hi))
    const = lambda shape: pl.BlockSpec(shape, lambda bi, hi: (0,) * len(shape))
    kern = functools.partial(_attn_kernel, lam_init=lam_init, tq=ATT_TQ)
    return pl.pallas_call(
        kern,
        grid=(b, ATT_HEADS),
        in_specs=[pl.BlockSpec(memory_space=pltpu.SMEM),
                  blk(Q_BLK), blk(K_BLK), blk(V_BLK), blk(GA_BLK),
                  const((s, V_DIM)), const((s, V_DIM)), const((4, HEAD_DIM)), const((1, V_DIM))],
        out_specs=pl.BlockSpec((None, s, V_DIM), lambda bi, hi: (bi, 0, hi)),
        out_shape=jax.ShapeDtypeStruct((b, s, ATT_HEADS * V_DIM), BF16),
        scratch_shapes=[pltpu.VMEM((s, V_DIM), BF16)] * 4 + [pltpu.VMEM((V_DIM + VT_PAD, s), BF16),
                                                             pltpu.VMEM((ATT_LOOKAHEAD + 1, s, ATT_TQ), F32)],
        compiler_params=pltpu.CompilerParams(
            dimension_semantics=("parallel", "parallel"), vmem_limit_bytes=VMEM_LIMIT),
        name="diff_attention",
    )(slopes, proj3, proj3, proj3, proj3, _kpos_lanes(s, HEAD_DIM), _kpos_lanes(s, 0), lamv, subln_g)


def _lru_kernel(xb_ref, gb_ref, cw_ref, cb_ref, wr_ref, br_ref, wi_ref, bi_ref, lam_ref, o_ref,
                xpad_ref, a_ref, u_ref, hs_ref, h_ref):
    ts, c = xb_ref.shape
    halo = 8

    @pl.when(pl.program_id(1) == 0)
    def _():
        xpad_ref[0:halo, :] = jnp.zeros((halo, c), F32)
        h_ref[...] = jnp.zeros_like(h_ref)

    xpad_ref[halo:, :] = xb_ref[...].astype(F32)
    xp = xpad_ref[...]
    xc = cb_ref[...] + cw_ref[CONV_WIDTH - 1:CONV_WIDTH, :] * xp[halo:, :]
    for d in range(1, CONV_WIDTH):
        xc = xc + cw_ref[CONV_WIDTH - 1 - d:CONV_WIDTH - d, :] * pltpu.roll(xp, d, axis=0)[halo:, :]
    xpad_ref[0:halo, :] = xpad_ref[ts:ts + halo, :]

    z = -lam_ref[...]
    softplus = jnp.maximum(z, 0.0) + jnp.log1p(jnp.exp(-jnp.abs(z)))
    decay2 = (-LRU_C * LOG2E) * softplus
    for n in range(LRU_BLOCKS):
        sl = slice(n * LRU_BLOCK_W, (n + 1) * LRU_BLOCK_W)
        xn = xc[:, sl]
        xn16 = xn.astype(BF16)
        r = _sigmoid(jnp.dot(xn16, wr_ref[n], preferred_element_type=F32) + br_ref[:, sl])
        ig = _sigmoid(jnp.dot(xn16, wi_ref[n], preferred_element_type=F32) + bi_ref[:, sl])
        a = jnp.exp2(r * decay2[:, sl])
        a_ref[:, sl] = a
        v = 1.0 - a * a
        u_ref[:, sl] = jnp.where(v > 0.0, v * lax.rsqrt(v), 0.0) * (ig * xn)

    row = lax.broadcasted_iota(jnp.int32, (8, c), 0)
    keep = [row >= d for d in (1, 2, 4)]

    def tile(t8, h):
        base = pl.multiple_of(t8 * 8, 8)
        a = a_ref[pl.ds(base, 8), :]
        u = u_ref[pl.ds(base, 8), :]
        for d, k in zip((1, 2, 4), keep):
            u = a * jnp.where(k, pltpu.roll(u, d, axis=0), 0.0) + u
            a = a * jnp.where(k, pltpu.roll(a, d, axis=0), 1.0)
        hs = a * h + u
        hs_ref[pl.ds(base, 8), :] = hs
        return hs[7:8, :]

    h_ref[...] = lax.fori_loop(0, ts // 8, tile, h_ref[...], unroll=LRU_SCAN_UNROLL)
    o_ref[...] = (hs_ref[...] * _silu(gb_ref[...].astype(F32))).astype(o_ref.dtype)


def _lru(proj3, conv_w, conv_b, w_r, b_r, w_i, b_i, lru_lambda):
    b, s, _ = proj3.shape
    c = SEG_W
    ts = LRU_TS
    row = lambda a: a.reshape(1, c)
    const = lambda shape: pl.BlockSpec(shape, lambda bi, si: (0,) * len(shape))
    return pl.pallas_call(
        _lru_kernel,
        grid=(b, s // ts),
        in_specs=[pl.BlockSpec((None, ts, c), lambda bi, si: (bi, si, XB_BLK)),
                  pl.BlockSpec((None, ts, c), lambda bi, si: (bi, si, GB_BLK)),
                  const((CONV_WIDTH, c)), const((1, c)),
                  const((LRU_BLOCKS, LRU_BLOCK_W, LRU_BLOCK_W)), const((1, c)),
                  const((LRU_BLOCKS, LRU_BLOCK_W, LRU_BLOCK_W)), const((1, c)),
                  const((1, c))],
        out_specs=pl.BlockSpec((None, ts, c), lambda bi, si: (bi, si, 0)),
        out_shape=jax.ShapeDtypeStruct((b, s, c), BF16),
        scratch_shapes=[pltpu.VMEM((ts + 8, c), F32), pltpu.VMEM((ts, c), F32),
                        pltpu.VMEM((ts, c), F32), pltpu.VMEM((ts, c), F32), pltpu.VMEM((1, c), F32)],
        compiler_params=pltpu.CompilerParams(
            dimension_semantics=("parallel", "arbitrary"), vmem_limit_bytes=VMEM_LIMIT),
        name="rg_lru",
    )(proj3, proj3, conv_w, row(conv_b), w_r.astype(BF16), row(b_r), w_i.astype(BF16), row(b_i),
      row(lru_lambda))


def _outproj_kernel(att_ref, rec_ref, ma_ref, mb_ref, x_ref, wpa_ref, wpb_ref, wout_ref,
                    g_ref, b_ref, o_ref):
    for r0 in range(0, x_ref.shape[0], OUT_SUB):
        rows = slice(r0, r0 + OUT_SUB)
        pa = jnp.dot(att_ref[rows, :], wpa_ref[...], preferred_element_type=F32)
        pb = jnp.dot(rec_ref[rows, :], wpb_ref[...], preferred_element_type=F32)
        merged = (_sigmoid(ma_ref[rows, :].astype(F32)) * pa + _sigmoid(mb_ref[rows, :].astype(F32)) * pb)
        out = jnp.dot(merged.astype(BF16), wout_ref[...], preferred_element_type=F32)
        y = DN_ALPHA * x_ref[rows, :] + out
        mu = jnp.mean(y, axis=1, keepdims=True)
        var = jnp.mean(y * y, axis=1, keepdims=True) - mu * mu
        o_ref[rows, :] = (y - mu) * lax.rsqrt(var + LN_EPS) * g_ref[...] + b_ref[...]


def _outproj(att2, rec2, gates2, x2, w_pa, w_pb, w_out, ln_g, ln_b):
    m, d = x2.shape
    e = att2.shape[1]
    tm = OUT_TM
    resident = lambda shape: pl.BlockSpec(shape, lambda i: (0,) * len(shape), pipeline_mode=pl.Buffered(1))
    return pl.pallas_call(
        _outproj_kernel,
        grid=(m // tm,),
        in_specs=[pl.BlockSpec((tm, e), lambda i: (i, 0)),
                  pl.BlockSpec((tm, e), lambda i: (i, 0)),
                  pl.BlockSpec((tm, d), lambda i: (i, 0)),
                  pl.BlockSpec((tm, d), lambda i: (i, 1)),
                  pl.BlockSpec((tm, d), lambda i: (i, 0)),
                  resident((e, d)), resident((e, d)), resident((d, d)),
                  resident((1, d)), resident((1, d))],
        out_specs=pl.BlockSpec((tm, d), lambda i: (i, 0)),
        out_shape=jax.ShapeDtypeStruct((m, d), F32),
        compiler_params=pltpu.CompilerParams(
            dimension_semantics=("parallel",), vmem_limit_bytes=VMEM_LIMIT),
        name="outproj_ln",
    )(att2, rec2, gates2, gates2, x2, w_pa, w_pb, w_out, ln_g.reshape(1, d), ln_b.reshape(1, d))


def kernel(x, w_in, conv_w, conv_b, w_rgate, b_rgate, w_igate, b_igate, lru_lambda,
           lam_q1, lam_k1, lam_q2, lam_k2, subln_g, w_pa, w_pb, w_out, ln_g, ln_b):
    b, s, d = x.shape
    m = b * s
    slopes = jnp.exp2(-(8.0 / ATT_HEADS) * jnp.arange(1, ATT_HEADS + 1, dtype=F32))
    x2 = x.reshape(m, d)
    for l in range(DEPTH):
        lam_init = 0.8 - 0.6 * math.exp(-0.3 * l)
        proj2 = _inproj(x2, w_in[l, :, :MERGE_COL0].astype(BF16))
        gates2 = _inproj8(x2, *_quantize_cols(w_in[l], MERGE_COL0))
        proj3 = proj2.reshape(b, s, MERGE_COL0)
        lamv = jnp.stack([lam_q1[l], lam_k1[l], lam_q2[l], lam_k2[l]]).astype(F32)
        att = _attention(proj3, slopes, lamv, subln_g[l].reshape(1, V_DIM).astype(F32), lam_init)
        rec = _lru(proj3, conv_w[l], conv_b[l], w_rgate[l], b_rgate[l], w_igate[l], b_igate[l],
                   lru_lambda[l])
        x2 = _outproj(att.reshape(m, -1), rec.reshape(m, -1), gates2, x2,
                      w_pa[l].astype(BF16), w_pb[l].astype(BF16), w_out[l].astype(BF16),
                      ln_g[l], ln_b[l])
    return x2.reshape(b, s, d)
```

```python
import functools
import math

import jax
import jax.numpy as jnp
from jax import lax
from jax.experimental import pallas as pl
from jax.experimental.pallas import tpu as pltpu

F32 = jnp.float32
BF16 = jnp.bfloat16

DEPTH = 2
ATT_HEADS = 8
HEAD_DIM = 64
V_DIM = 2 * HEAD_DIM
LRU_BLOCKS = 8
LRU_BLOCK_W = 128
CONV_WIDTH = 4
LRU_C = 8.0
LN_EPS = 1e-5
SUBLN_EPS = 1e-5
DN_ALPHA = (2 * DEPTH) ** 0.25
LOG2E = math.log2(math.e)
KPOS_RADIX = 16
VT_PAD = 16

SEG_W = 1024
Q_BLK, K_BLK, V_BLK, GA_BLK, XB_BLK, GB_BLK = 0, 1, 2, 3, 4, 5
MERGE_COL0 = 6 * SEG_W
MA_BLK2, MB_BLK2 = 3, 4

FP8 = jnp.float8_e4m3fn
FP8_MAX = 448.0
PREP_TN = 512

VMEM_LIMIT = 56 * 1024 * 1024

INPROJ_TM, INPROJ_TN = 1024, 1024
ATT_TQ = 512
ATT_LOOKAHEAD = 2
LRU_TS = 512
LRU_SCAN_UNROLL = 2
OUT_TM = 512
OUT_SUB = 256


def _sigmoid(z):
    return 0.5 * jnp.tanh(0.5 * z) + 0.5


def _silu(z):
    hz = 0.5 * z
    return hz * jnp.tanh(hz) + hz


def _cast_kernel(w_ref, o_ref):
    o_ref[...] = w_ref[...].astype(o_ref.dtype)


def _cast_bf16(w, n_cols=None):
    layers, k, n_all = w.shape
    n = n_all if n_cols is None else n_cols
    tn = PREP_TN
    spec = pl.BlockSpec((None, k, tn), lambda l, j: (l, 0, j))
    return pl.pallas_call(
        _cast_kernel,
        grid=(layers, n // tn),
        in_specs=[spec],
        out_specs=spec,
        out_shape=jax.ShapeDtypeStruct((layers, k, n), BF16),
        compiler_params=pltpu.CompilerParams(
            dimension_semantics=("parallel", "parallel"), vmem_limit_bytes=VMEM_LIMIT),
        name="cast_bf16",
    )(w)


def _quantize_cols_kernel(w_ref, q_ref, s_ref):
    w = w_ref[...]
    amax = jnp.max(jnp.abs(w), axis=0, keepdims=True)
    scale = jnp.where(amax > 0.0, amax * (1.0 / FP8_MAX), 1.0)
    s_ref[...] = scale
    q_ref[...] = (w * (1.0 / scale)).astype(FP8)


def _quantize_cols(w, col0):
    layers, k, n_all = w.shape
    tn = PREP_TN
    blk0 = col0 // tn
    n = n_all - col0
    return pl.pallas_call(
        _quantize_cols_kernel,
        grid=(layers, n // tn),
        in_specs=[pl.BlockSpec((None, k, tn), lambda l, j: (l, 0, blk0 + j))],
        out_specs=[pl.BlockSpec((None, k, tn), lambda l, j: (l, 0, j)),
                   pl.BlockSpec((None, 1, tn), lambda l, j: (l, 0, j))],
        out_shape=[jax.ShapeDtypeStruct((layers, k, n), FP8), jax.ShapeDtypeStruct((layers, 1, n), F32)],
        compiler_params=pltpu.CompilerParams(
            dimension_semantics=("parallel", "parallel"), vmem_limit_bytes=VMEM_LIMIT),
        name="quantize_cols",
    )(w)


def _inproj_kernel(x_ref, w16_ref, w8_ref, sw_ref, o_ref, xb_ref, x8_ref, sx_ref, *, n16):
    j = pl.program_id(1)

    @pl.when(j == 0)
    def _():
        x = x_ref[...]
        xb_ref[...] = x.astype(BF16)
        amax = jnp.max(jnp.abs(x), axis=1, keepdims=True)
        scale = jnp.where(amax > 0.0, amax * (1.0 / FP8_MAX), 1.0)
        sx_ref[...] = scale
        x8_ref[...] = (x * (1.0 / scale)).astype(FP8)

    @pl.when(j < n16)
    def _():
        o_ref[...] = jnp.dot(xb_ref[...], w16_ref[...], preferred_element_type=F32).astype(o_ref.dtype)

    @pl.when(j >= n16)
    def _():
        acc = jnp.dot(x8_ref[...], w8_ref[...], preferred_element_type=F32)
        o_ref[...] = (acc * sx_ref[...] * sw_ref[...]).astype(o_ref.dtype)


def _inproj(x2, w16, w8, sw, layer):
    m, k = x2.shape
    tm, tn = INPROJ_TM, INPROJ_TN
    n16 = w16.shape[2] // tn
    n8 = w8.shape[2] // tn
    return pl.pallas_call(
        functools.partial(_inproj_kernel, n16=n16),
        grid=(m // tm, n16 + n8),
        in_specs=[pl.BlockSpec((tm, k), lambda i, j: (i, 0)),
                  pl.BlockSpec((None, k, tn), lambda i, j: (layer, 0, jnp.minimum(j, n16 - 1))),
                  pl.BlockSpec((None, k, tn), lambda i, j: (layer, 0, jnp.maximum(j - n16, 0))),
                  pl.BlockSpec((None, 1, tn), lambda i, j: (layer, 0, jnp.maximum(j - n16, 0)))],
        out_specs=pl.BlockSpec((tm, tn), lambda i, j: (i, j)),
        out_shape=jax.ShapeDtypeStruct((m, (n16 + n8) * tn), BF16),
        scratch_shapes=[pltpu.VMEM((tm, k), BF16), pltpu.VMEM((tm, k), FP8), pltpu.VMEM((tm, 1), F32)],
        compiler_params=pltpu.CompilerParams(
            dimension_semantics=("parallel", "arbitrary"), vmem_limit_bytes=VMEM_LIMIT),
        name="inproj",
    )(x2, w16, w8, sw)


def _attn_kernel(slopes_ref, q_ref, k_ref, v_ref, ga_ref, kpos1_ref, kpos2_ref, lamv_ref, g_ref, o_ref,
                 qa1_ref, ka1_ref, qa2_ref, ka2_ref, vt_ref, s_ref, *, lam_init, tq):
    s_len = q_ref.shape[0]
    slope2 = jnp.full((1, V_DIM), slopes_ref[pl.program_id(1)] * LOG2E, F32)
    s1 = slope2.astype(BF16).astype(F32)
    s2 = (slope2 - s1).astype(BF16).astype(F32)
    s3 = slope2 - s1 - s2
    lane_row = lax.broadcasted_iota(jnp.int32, (1, V_DIM), 1)

    def slope_lanes(base):
        r = jnp.zeros((1, V_DIM), F32)
        for idx, piece in enumerate((s1, s2, s3)):
            r = jnp.where(lane_row == base + 2 * idx, float(KPOS_RADIX) * piece, r)
            r = jnp.where(lane_row == base + 2 * idx + 1, piece, r)
        return r

    first = lax.broadcasted_iota(jnp.int32, (s_len, V_DIM), 1) < HEAD_DIM
    q = q_ref[...].astype(F32) * (HEAD_DIM ** -0.5 * LOG2E)
    qa1_ref[...] = jnp.where(first, q, slope_lanes(HEAD_DIM)).astype(BF16)
    qa2_ref[...] = jnp.where(first, slope_lanes(0), q).astype(BF16)
    k = k_ref[...]
    ka1_ref[...] = jnp.where(first, k, kpos1_ref[...])
    ka2_ref[...] = jnp.where(first, kpos2_ref[...], k)
    vt_ref[0:V_DIM, :] = v_ref[...].astype(F32).T.astype(BF16)
    ones_row = lax.broadcasted_iota(jnp.int32, (VT_PAD, s_len), 0) == 0
    vt_ref[V_DIM:, :] = jnp.where(ones_row, 1.0, 0.0).astype(BF16)

    lamv = lamv_ref[...]
    lam = (jnp.exp(jnp.sum(lamv[0:1] * lamv[1:2], axis=1, keepdims=True))
           - jnp.exp(jnp.sum(lamv[2:3] * lamv[3:4], axis=1, keepdims=True)) + lam_init)

    causal = (lax.broadcasted_iota(jnp.int32, (tq, tq), 0)
              <= lax.broadcasted_iota(jnp.int32, (tq, tq), 1))
    nt = (((1,), (1,)), ((), ()))

    def scores(qa_ref, ka_ref, r0, slot):
        qa = qa_ref[r0:r0 + tq, :]
        m = None
        for c0 in range(0, r0 + tq, tq):
            s = lax.dot_general(ka_ref[c0:c0 + tq, :], qa, nt, preferred_element_type=F32)
            if c0 == r0:
                s = jnp.where(causal, s, -jnp.inf)
            s_ref[slot, c0:c0 + tq, :] = s
            cm = jnp.max(s, axis=0, keepdims=True)
            m = cm if m is None else jnp.maximum(m, cm)
        return m

    def weighted_values(m, r0, slot):
        acc = None
        for c0 in range(0, r0 + tq, tq):
            p = jnp.exp2(s_ref[slot, c0:c0 + tq, :] - m).astype(BF16)
            part = jnp.dot(vt_ref[:, c0:c0 + tq], p, preferred_element_type=F32)
            acc = part if acc is None else acc + part
        return acc[0:V_DIM] / acc[V_DIM:V_DIM + 1]

    units = [(qi * tq, mp) for qi in range(s_len // tq) for mp in range(2)]
    refs = ((qa1_ref, ka1_ref), (qa2_ref, ka2_ref))
    gain = g_ref[...] * (1.0 - lam_init)
    n_slots = ATT_LOOKAHEAD + 1
    pending = [scores(*refs[mp], r0, i) for i, (r0, mp) in enumerate(units[:ATT_LOOKAHEAD])]
    first_map = None
    for idx, (r0, mp) in enumerate(units):
        ahead = idx + ATT_LOOKAHEAD
        if ahead < len(units):
            nr0, nmp = units[ahead]
            pending.append(scores(*refs[nmp], nr0, ahead % n_slots))
        o_map = weighted_values(pending.pop(0), r0, idx % n_slots)
        if mp == 0:
            first_map = o_map
            continue
        ot = first_map - lam * o_map
        ot = ot * lax.rsqrt(jnp.mean(ot * ot, axis=0, keepdims=True) + SUBLN_EPS)
        ga = ga_ref[r0:r0 + tq, :].astype(F32)
        o_ref[r0:r0 + tq, :] = (ot.T * gain * _silu(ga)).astype(o_ref.dtype)


def _kpos_lanes(s_len, base):
    pos = jnp.arange(s_len, dtype=jnp.int32)[:, None]
    lane = jnp.arange(V_DIM, dtype=jnp.int32)[None, :] - base
    digit = jnp.where(lane % 2 == 0, pos // KPOS_RADIX, pos % KPOS_RADIX)
    return jnp.where((lane >= 0) & (lane < 6), digit, 0).astype(BF16)


def _attention(proj3, slopes, lamv, subln_g, lam_init):
    b, s, _ = proj3.shape
    per_head = SEG_W // V_DIM
    blk = lambda seg: pl.BlockSpec((None, s, V_DIM), lambda bi, hi, seg=seg: (bi, 0, seg * per_head + hi))
    const = lambda shape: pl.BlockSpec(shape, lambda bi, hi: (0,) * len(shape))
    kern = functools.partial(_attn_kernel, lam_init=lam_init, tq=ATT_TQ)
    return pl.pallas_call(
        kern,
        grid=(b, ATT_HEADS),
        in_specs=[pl.BlockSpec(memory_space=pltpu.SMEM),
                  blk(Q_BLK), blk(K_BLK), blk(V_BLK), blk(GA_BLK),
                  const((s, V_DIM)), const((s, V_DIM)), const((4, HEAD_DIM)), const((1, V_DIM))],
        out_specs=pl.BlockSpec((None, s, V_DIM), lambda bi, hi: (bi, 0, hi)),
        out_shape=jax.ShapeDtypeStruct((b, s, ATT_HEADS * V_DIM), BF16),
        scratch_shapes=[pltpu.VMEM((s, V_DIM), BF16)] * 4 + [pltpu.VMEM((V_DIM + VT_PAD, s), BF16),
                                                             pltpu.VMEM((ATT_LOOKAHEAD + 1, s, ATT_TQ), F32)],
        compiler_params=pltpu.CompilerParams(
            dimension_semantics=("parallel", "parallel"), vmem_limit_bytes=VMEM_LIMIT),
        name="diff_attention",
    )(slopes, proj3, proj3, proj3, proj3, _kpos_lanes(s, HEAD_DIM), _kpos_lanes(s, 0), lamv, subln_g)


def _lru_kernel(xb_ref, gb_ref, cw_ref, cb_ref, wr_ref, br_ref, wi_ref, bi_ref, lam_ref, o_ref,
                xpad_ref, a_ref, u_ref, hs_ref, h_ref):
    ts, c = xb_ref.shape
    halo = 8

    @pl.when(pl.program_id(1) == 0)
    def _():
        xpad_ref[0:halo, :] = jnp.zeros((halo, c), F32)
        h_ref[...] = jnp.zeros_like(h_ref)

    xpad_ref[halo:, :] = xb_ref[...].astype(F32)
    xp = xpad_ref[...]
    xc = cb_ref[...] + cw_ref[CONV_WIDTH - 1:CONV_WIDTH, :] * xp[halo:, :]
    for d in range(1, CONV_WIDTH):
        xc = xc + cw_ref[CONV_WIDTH - 1 - d:CONV_WIDTH - d, :] * pltpu.roll(xp, d, axis=0)[halo:, :]
    xpad_ref[0:halo, :] = xpad_ref[ts:ts + halo, :]

    z = -lam_ref[...]
    softplus = jnp.maximum(z, 0.0) + jnp.log1p(jnp.exp(-jnp.abs(z)))
    decay2 = (-LRU_C * LOG2E) * softplus
    for n in range(LRU_BLOCKS):
        sl = slice(n * LRU_BLOCK_W, (n + 1) * LRU_BLOCK_W)
        xn = xc[:, sl]
        xn16 = xn.astype(BF16)
        r = _sigmoid(jnp.dot(xn16, wr_ref[n], preferred_element_type=F32) + br_ref[:, sl])
        ig = _sigmoid(jnp.dot(xn16, wi_ref[n], preferred_element_type=F32) + bi_ref[:, sl])
        a = jnp.exp2(r * decay2[:, sl])
        a_ref[:, sl] = a
        v = 1.0 - a * a
        u_ref[:, sl] = jnp.where(v > 0.0, v * lax.rsqrt(v), 0.0) * (ig * xn)

    row = lax.broadcasted_iota(jnp.int32, (8, c), 0)
    keep = [row >= d for d in (1, 2, 4)]

    def tile(t8, h):
        base = pl.multiple_of(t8 * 8, 8)
        a = a_ref[pl.ds(base, 8), :]
        u = u_ref[pl.ds(base, 8), :]
        for d, k in zip((1, 2, 4), keep):
            u = a * jnp.where(k, pltpu.roll(u, d, axis=0), 0.0) + u
            a = a * jnp.where(k, pltpu.roll(a, d, axis=0), 1.0)
        hs = a * h + u
        hs_ref[pl.ds(base, 8), :] = hs
        return hs[7:8, :]

    h_ref[...] = lax.fori_loop(0, ts // 8, tile, h_ref[...], unroll=LRU_SCAN_UNROLL)
    o_ref[...] = (hs_ref[...] * _silu(gb_ref[...].astype(F32))).astype(o_ref.dtype)


def _lru(proj3, conv_w, conv_b, w_r, b_r, w_i, b_i, lru_lambda):
    b, s, _ = proj3.shape
    c = SEG_W
    ts = LRU_TS
    row = lambda a: a.reshape(1, c)
    const = lambda shape: pl.BlockSpec(shape, lambda bi, si: (0,) * len(shape))
    return pl.pallas_call(
        _lru_kernel,
        grid=(b, s // ts),
        in_specs=[pl.BlockSpec((None, ts, c), lambda bi, si: (bi, si, XB_BLK)),
                  pl.BlockSpec((None, ts, c), lambda bi, si: (bi, si, GB_BLK)),
                  const((CONV_WIDTH, c)), const((1, c)),
                  const((LRU_BLOCKS, LRU_BLOCK_W, LRU_BLOCK_W)), const((1, c)),
                  const((LRU_BLOCKS, LRU_BLOCK_W, LRU_BLOCK_W)), const((1, c)),
                  const((1, c))],
        out_specs=pl.BlockSpec((None, ts, c), lambda bi, si: (bi, si, 0)),
        out_shape=jax.ShapeDtypeStruct((b, s, c), BF16),
        scratch_shapes=[pltpu.VMEM((ts + 8, c), F32), pltpu.VMEM((ts, c), F32),
                        pltpu.VMEM((ts, c), F32), pltpu.VMEM((ts, c), F32), pltpu.VMEM((1, c), F32)],
        compiler_params=pltpu.CompilerParams(
            dimension_semantics=("parallel", "arbitrary"), vmem_limit_bytes=VMEM_LIMIT),
        name="rg_lru",
    )(proj3, proj3, conv_w, row(conv_b), w_r.astype(BF16), row(b_r), w_i.astype(BF16), row(b_i),
      row(lru_lambda))


def _outproj_kernel(att_ref, rec_ref, ma_ref, mb_ref, x_ref, wpa_ref, wpb_ref, wout_ref,
                    g_ref, b_ref, o_ref):
    for r0 in range(0, x_ref.shape[0], OUT_SUB):
        rows = slice(r0, r0 + OUT_SUB)
        pa = jnp.dot(att_ref[rows, :], wpa_ref[...], preferred_element_type=F32)
        pb = jnp.dot(rec_ref[rows, :], wpb_ref[...], preferred_element_type=F32)
        merged = (_sigmoid(ma_ref[rows, :].astype(F32)) * pa + _sigmoid(mb_ref[rows, :].astype(F32)) * pb)
        out = jnp.dot(merged.astype(BF16), wout_ref[...], preferred_element_type=F32)
        y = DN_ALPHA * x_ref[rows, :] + out
        mu = jnp.mean(y, axis=1, keepdims=True)
        var = jnp.mean(y * y, axis=1, keepdims=True) - mu * mu
        o_ref[rows, :] = (y - mu) * lax.rsqrt(var + LN_EPS) * g_ref[...] + b_ref[...]


def _outproj(att2, rec2, proj2, x2, w_pa, w_pb, w_out, ln_g, ln_b, layer):
    m, d = x2.shape
    e = att2.shape[1]
    tm = OUT_TM
    resident = lambda shape: pl.BlockSpec(shape, lambda i: (0,) * len(shape), pipeline_mode=pl.Buffered(1))
    weight = lambda shape: pl.BlockSpec((None,) + shape, lambda i: (layer, 0, 0), pipeline_mode=pl.Buffered(1))
    return pl.pallas_call(
        _outproj_kernel,
        grid=(m // tm,),
        in_specs=[pl.BlockSpec((tm, e), lambda i: (i, 0)),
                  pl.BlockSpec((tm, e), lambda i: (i, 0)),
                  pl.BlockSpec((tm, d), lambda i: (i, MA_BLK2)),
                  pl.BlockSpec((tm, d), lambda i: (i, MB_BLK2)),
                  pl.BlockSpec((tm, d), lambda i: (i, 0)),
                  weight((e, d)), weight((e, d)), weight((d, d)),
                  resident((1, d)), resident((1, d))],
        out_specs=pl.BlockSpec((tm, d), lambda i: (i, 0)),
        out_shape=jax.ShapeDtypeStruct((m, d), F32),
        compiler_params=pltpu.CompilerParams(
            dimension_semantics=("parallel",), vmem_limit_bytes=VMEM_LIMIT),
        name="outproj_ln",
    )(att2, rec2, proj2, proj2, x2, w_pa, w_pb, w_out, ln_g.reshape(1, d), ln_b.reshape(1, d))


def kernel(x, w_in, conv_w, conv_b, w_rgate, b_rgate, w_igate, b_igate, lru_lambda,
           lam_q1, lam_k1, lam_q2, lam_k2, subln_g, w_pa, w_pb, w_out, ln_g, ln_b):
    b, s, d = x.shape
    m = b * s
    slopes = jnp.exp2(-(8.0 / ATT_HEADS) * jnp.arange(1, ATT_HEADS + 1, dtype=F32))
    x2 = x.reshape(m, d)
    w_in16 = _cast_bf16(w_in, MERGE_COL0)
    w_in8, w_in8_scale = _quantize_cols(w_in, MERGE_COL0)
    w_pa16, w_pb16, w_out16 = _cast_bf16(w_pa), _cast_bf16(w_pb), _cast_bf16(w_out)
    for l in range(DEPTH):
        lam_init = 0.8 - 0.6 * math.exp(-0.3 * l)
        proj2 = _inproj(x2, w_in16, w_in8, w_in8_scale, l)
        proj3 = proj2.reshape(b, s, proj2.shape[1])
        lamv = jnp.stack([lam_q1[l], lam_k1[l], lam_q2[l], lam_k2[l]]).astype(F32)
        att = _attention(proj3, slopes, lamv, subln_g[l].reshape(1, V_DIM).astype(F32), lam_init)
        rec = _lru(proj3, conv_w[l], conv_b[l], w_rgate[l], b_rgate[l], w_igate[l], b_igate[l],
                   lru_lambda[l])
        x2 = _outproj(att.reshape(m, -1), rec.reshape(m, -1), proj2, x2, w_pa16, w_pb16, w_out16,
                      ln_g[l], ln_b[l], l)
    return x2.reshape(b, s, d)
```
